```python
import math
import jax
import jax.numpy as jnp
from jax import lax
import numpy as np

D_MODEL = 2048
BATCH = 2
SEQ = 4096
DEPTH = 4

GRID_W = 64
CTX_LEN = 256
D_GROUP = D_MODEL // 2
D_MIX = 2 * D_GROUP

NA_HEADS = 8
NA_HEAD_DIM = D_GROUP // NA_HEADS
NA_WIN_H = 8
NA_WIN_W = 16

HY_WIDTH = D_GROUP
HY_EMB_DIM = 33
HY_BANDS = (HY_EMB_DIM - 1) // 2
HY_FILTER_HIDDEN = 64
HY_MIN_DECAY = math.log(1e-2) / 1.5
HY_MAX_DECAY = math.log(1e-2) / 0.3

RW_WIDTH = D_GROUP
RW_HEAD = 64
RW_HEADS = RW_WIDTH // RW_HEAD
RW_DECAY_RANK = max(32, int(round(D_MODEL ** 0.5 * 1.8 / 32)) * 32)
RW_ICL_RANK = RW_DECAY_RANK
RW_GATE_RANK = max(32, int(round(D_MODEL ** 0.8 * 0.6 / 32)) * 32)
RW_LR = RW_DECAY_RANK + RW_ICL_RANK + RW_GATE_RANK
RW_GN_EPS = 64e-5

SC_WIDTH = D_GROUP

MLP_HIDDEN = 4 * D_MODEL
EVEN_IN = 3 * D_GROUP + 3 * HY_WIDTH
ODD_IN = 3 * RW_WIDTH + 2 * RW_LR + 3 * SC_WIDTH
N_EVEN = (DEPTH + 1) // 2
N_ODD = DEPTH // 2
DN_ALPHA = (2.0 * DEPTH) ** 0.25
DN_BETA = (8.0 * DEPTH) ** -0.25
LN_EPS = 1e-5
NEG_INF = -1e30

kernel_name = 'hybrid_diffusion_trunk'


def layer_norm(x, g, b):
    xf = x.astype(jnp.float32)
    mu = jnp.mean(xf, axis=-1, keepdims=True)
    var = jnp.mean(jnp.square(xf - mu), axis=-1, keepdims=True)
    return ((xf - mu) * lax.rsqrt(var + LN_EPS) * g + b).astype(x.dtype)


def conv3(x, w):
    xp = jnp.pad(x, ((0, 0), (1, 1), (0, 0)))
    return xp[:, :-2] * w[0] + xp[:, 1:-1] * w[1] + xp[:, 2:] * w[2]


def token_shift(z, reverse):
    if reverse:
        return jnp.pad(z[:, 1:], ((0, 0), (0, 1), (0, 0)))
    return jnp.pad(z[:, :-1], ((0, 0), (1, 0), (0, 0)))


def window_starts(n, k):
    return jnp.clip(jnp.arange(n) - k // 2, 0, n - k)


def dense_attention(q, k, v):
    b, n, h, dh = q.shape
    s = jnp.einsum('bqhd,bkhd->bhqk', q, k, preferred_element_type=jnp.float32) * dh ** -0.5
    p = jax.nn.softmax(s, axis=-1).astype(v.dtype)
    return jnp.einsum('bhqk,bkhd->bqhd', p, v).reshape(b, n, h * dh)


def neighbourhood_attention(q, k, v, k_ctx, v_ctx, rpb):
    b, n, h, dh = q.shape
    rows = n // GRID_W
    kh = min(NA_WIN_H, rows)
    scale = dh ** -0.5
    qg = q.reshape(b, rows, GRID_W, h, dh)
    kg = k.reshape(b, rows, GRID_W, h, dh)
    vg = v.reshape(b, rows, GRID_W, h, dh)
    row_idx = window_starts(rows, kh)[:, None] + jnp.arange(kh)[None, :]
    k_band = kg[:, row_idx].reshape(b, rows, kh * GRID_W, h, dh)
    v_band = vg[:, row_idx].reshape(b, rows, kh * GRID_W, h, dh)
    cols = jnp.arange(GRID_W)
    col0 = window_starts(GRID_W, NA_WIN_W)
    col_ok = (cols[None, :] >= col0[:, None]) & (cols[None, :] < col0[:, None] + NA_WIN_W)
    d_row = row_idx - jnp.arange(rows)[:, None] + (NA_WIN_H - 1)
    d_col = jnp.clip(cols[None, :] - cols[:, None] + (NA_WIN_W - 1), 0, 2 * NA_WIN_W - 2)
    bias = rpb[:, d_row[:, None, :, None], d_col[None, :, None, :]].astype(jnp.float32)
    bias = jnp.where(col_ok[None, None, :, None, :], bias, NEG_INF).reshape(h, rows, GRID_W, kh * GRID_W)
    s_loc = jnp.einsum('brqhd,brkhd->bhrqk', qg, k_band, preferred_element_type=jnp.float32) * scale + bias[None]
    s_ctx = jnp.einsum('brqhd,bkhd->bhrqk', qg, k_ctx, preferred_element_type=jnp.float32) * scale
    p = jax.nn.softmax(jnp.concatenate([s_loc, s_ctx], axis=-1), axis=-1).astype(v.dtype)
    n_loc = kh * GRID_W
    o = (jnp.einsum('bhrqk,brkhd->brqhd', p[..., :n_loc], v_band)
         + jnp.einsum('bhrqk,bkhd->brqhd', p[..., n_loc:], v_ctx))
    return o.reshape(b, n, h * dh)


def hyena_filter(length, w1, b1, w2, b2, w3, b3, w4, freq):
    f32 = jnp.float32
    t = jnp.linspace(0.0, 1.0, length, dtype=f32)[:, None]
    bands = jnp.linspace(1e-4, HY_BANDS - 1, HY_BANDS, dtype=f32)
    ang = (2.0 * math.pi / length) * jnp.arange(length, dtype=f32)[:, None] * bands[None, :]
    z = jnp.concatenate([t, jnp.cos(ang), -jnp.sin(ang)], axis=-1)
    freq = freq.astype(f32)
    h = jnp.sin(freq[0] * (z @ w1.astype(f32) + b1.astype(f32)))
    h = jnp.sin(freq[1] * (h @ w2.astype(f32) + b2.astype(f32)))
    h = jnp.sin(freq[2] * (h @ w3.astype(f32) + b3.astype(f32)))
    h = h @ w4.astype(f32)
    decay = jnp.abs(jnp.linspace(HY_MIN_DECAY, HY_MAX_DECAY, HY_WIDTH, dtype=f32))
    window = jnp.exp(-t * decay[None, :])
    h_fwd = h[:, :HY_WIDTH] * window
    h_bwd = h[:, HY_WIDTH:] * window
    return jnp.concatenate([h_fwd, jnp.zeros((1, HY_WIDTH), f32), h_bwd[:0:-1]], axis=0)


def hyena(z, conv_w, conv_b, bias, filt):
    length = z.shape[1]
    u = conv3(z, conv_w) + conv_b
    x0, x1, v = jnp.split(u, 3, axis=-1)
    g = (v * x1).astype(jnp.float32)
    filt_2l = hyena_filter(length, *filt)
    y = jnp.fft.irfft(jnp.fft.rfft(g, n=2 * length, axis=1) * jnp.fft.rfft(filt_2l, axis=0)[None],
                      n=2 * length, axis=1)[:, :length]
    y = y + g * bias.astype(jnp.float32)
    return (y * x0.astype(jnp.float32)).astype(z.dtype)


def to_heads(t):
    return t.reshape(t.shape[0], t.shape[1], RW_HEADS, RW_HEAD)


def rwkv_inputs(z_shared, z_lr, mu, w0, w2, a0, a2, k_k, k_a, reverse):
    z = jnp.concatenate([z_shared, z_lr], axis=-1).astype(jnp.float32)
    z = z + mu * (token_shift(z, reverse) - z)
    wd = RW_WIDTH
    r, k, v = z[..., :wd], z[..., wd:2 * wd], z[..., 2 * wd:3 * wd]
    o = 3 * wd
    wl = z[..., o:o + RW_DECAY_RANK]
    al = z[..., o + RW_DECAY_RANK:o + RW_DECAY_RANK + RW_ICL_RANK]
    gl = z[..., o + RW_DECAY_RANK + RW_ICL_RANK:]
    w = -jax.nn.softplus(-(w0 + jnp.tanh(wl) @ w2)) - 0.5
    decay = jnp.exp(-jnp.exp(w))
    a = jax.nn.sigmoid(a0 + al @ a2)
    kk = to_heads(k * k_k)
    kk = kk / jnp.maximum(jnp.sqrt(jnp.sum(jnp.square(kk), axis=-1, keepdims=True)), 1e-12)
    k = k * (1.0 + (a - 1.0) * k_a)
    return to_heads(r), to_heads(k), to_heads(v), to_heads(decay), kk, kk * to_heads(a), gl


def rwkv_scan(state0, decay, kk, kk_a, k, v, r, reverse, emit):
    seq = lambda t: jnp.moveaxis(t, 1, 0)
    xs = (seq(decay), seq(kk), seq(kk_a), seq(k), seq(v)) + ((seq(r),) if emit else ())

    def step(S, inp):
        w_t, kk_t, b_t, k_t, v_t = inp[:5]
        sa = jnp.einsum('bhij,bhj->bhi', S, -kk_t)
        S = S * w_t[:, :, None, :] + sa[..., None] * b_t[:, :, None, :] + v_t[..., None] * k_t[:, :, None, :]
        y = jnp.einsum('bhij,bhj->bhi', S, inp[5]) if emit else None
        return S, y

    S, ys = lax.scan(step, state0, xs, reverse=reverse)
    return S, (jnp.moveaxis(ys, 0, 1) if emit else None)


def rwkv_output(y, r, k, v, g, r_k, lnx_g, lnx_b):
    b, n = y.shape[:2]
    mu = jnp.mean(y, axis=-1, keepdims=True)
    var = jnp.mean(jnp.square(y - mu), axis=-1, keepdims=True)
    yn = ((y - mu) * lax.rsqrt(var + RW_GN_EPS)).reshape(b, n, RW_WIDTH) * lnx_g + lnx_b
    bonus = (jnp.sum(r * k * r_k, axis=-1, keepdims=True) * v).reshape(b, n, RW_WIDTH)
    return (yn + bonus) * g.astype(jnp.float32)


def rwkv_bidir(zx, zc, mu, w0, w2, a0, a2, g2, k_k, k_a, r_k, lnx_g, lnx_b, need_ctx):
    w3 = 3 * RW_WIDTH
    out_x, out_c = 0.0, 0.0
    for d, reverse in enumerate((False, True)):
        lo = w3 + d * RW_LR
        prm = (mu[d], w0[d], w2[d], a0[d], a2[d], k_k, k_a, reverse)
        rc, kc, vc, wc, kkc, bc, gc = rwkv_inputs(zc[..., :w3], zc[..., lo:lo + RW_LR], *prm)
        s0 = jnp.zeros((zc.shape[0], RW_HEADS, RW_HEAD, RW_HEAD), jnp.float32)
        s_ctx, yc = rwkv_scan(s0, wc, kkc, bc, kc, vc, rc, reverse, need_ctx)
        rx, kx, vx, wx, kkx, bx, gx = rwkv_inputs(zx[..., :w3], zx[..., lo:lo + RW_LR], *prm)
        _, yx = rwkv_scan(s_ctx, wx, kkx, bx, kx, vx, rx, reverse, True)
        out_x = out_x + rwkv_output(yx, rx, kx, vx, jax.nn.sigmoid(gx) @ g2[d], r_k, lnx_g, lnx_b)
        if need_ctx:
            out_c = out_c + rwkv_output(yc, rc, kc, vc, jax.nn.sigmoid(gc) @ g2[d], r_k, lnx_g, lnx_b)
    return out_x.astype(zx.dtype), (out_c.astype(zc.dtype) if need_ctx else None)


def short_conv_mixer(z, w):
    bg, cg, h = jnp.split(z, 3, axis=-1)
    return bg * conv3(cg * h, w)


def sqrelu_mlp(h, w_up, w_down):
    return jnp.square(jax.nn.relu(h @ w_up)) @ w_down


def even_mixer(hx, hc, w_in, rpb, conv_w, conv_b, hy_bias, filt, need_ctx):
    zx = hx @ w_in
    zc = hc @ w_in
    split_a = 3 * D_GROUP

    def qkv(z):
        b, n = z.shape[:2]
        return [z[..., j * D_GROUP:(j + 1) * D_GROUP].reshape(b, n, NA_HEADS, NA_HEAD_DIM) for j in range(3)]

    qx, kx, vx = qkv(zx)
    qc, kc, vc = qkv(zc)
    yx = jnp.concatenate([neighbourhood_attention(qx, kx, vx, kc, vc, rpb),
                          hyena(zx[..., split_a:], conv_w, conv_b, hy_bias, filt)], axis=-1)
    if not need_ctx:
        return yx, None
    yc = jnp.concatenate([dense_attention(qc, kc, vc),
                          hyena(zc[..., split_a:], conv_w, conv_b, hy_bias, filt)], axis=-1)
    return yx, yc


def odd_mixer(hx, hc, w_in, mu, w0, w2, a0, a2, g2, k_k, k_a, r_k, lnx_g, lnx_b, sc_w, need_ctx):
    zx = hx @ w_in
    zc = hc @ w_in
    off = 3 * RW_WIDTH + 2 * RW_LR
    rx, rc = rwkv_bidir(zx[..., :off], zc[..., :off], mu, w0, w2, a0, a2, g2, k_k, k_a, r_k,
                        lnx_g, lnx_b, need_ctx)
    yx = jnp.concatenate([rx, short_conv_mixer(zx[..., off:], sc_w)], axis=-1)
    if not need_ctx:
        return yx, None
    return yx, jnp.concatenate([rc, short_conv_mixer(zc[..., off:], sc_w)], axis=-1)


def setup_inputs(seed: int = 0) -> dict:
    key = jax.random.key(seed)
    keys = iter(jax.random.split(key, 48))
    f32 = jnp.float32
    D = D_MODEL

    def nrm(shape, scale):
        return scale * jax.random.normal(next(keys), shape, f32)

    def unif(shape, lo, hi):
        return jax.random.uniform(next(keys), shape, f32, lo, hi)

    return {
        'x': nrm((BATCH, SEQ, D), 1.0),
        'c': nrm((BATCH, D), 1.0),
        'ctx': nrm((BATCH, CTX_LEN, D), 1.0),
        'c_ctx': nrm((D,), 1.0),
        'ada_w': nrm((DEPTH, D, 6 * D), 0.5 * D ** -0.5),
        'ada_b': nrm((DEPTH, 6 * D), 0.01),
        'ln1_g': 1.0 + nrm((DEPTH, D), 0.01),
        'ln1_b': nrm((DEPTH, D), 0.01),
        'ln2_g': 1.0 + nrm((DEPTH, D), 0.01),
        'ln2_b': nrm((DEPTH, D), 0.01),
        'mlp_w_up': nrm((DEPTH, D, MLP_HIDDEN), D ** -0.5),
        'mlp_w_down': nrm((DEPTH, MLP_HIDDEN, D), DN_BETA * MLP_HIDDEN ** -0.5),
        'w_out': nrm((DEPTH, D_MIX, D), DN_BETA * D_MIX ** -0.5),
        'even_w_in': nrm((N_EVEN, D, EVEN_IN), D ** -0.5),
        'na_rpb': nrm((N_EVEN, NA_HEADS, 2 * NA_WIN_H - 1, 2 * NA_WIN_W - 1), 0.1),
        'hy_conv_w': nrm((N_EVEN, 3, 3 * HY_WIDTH), 3 ** -0.5),
        'hy_conv_b': nrm((N_EVEN, 3 * HY_WIDTH), 0.02),
        'hy_w1': nrm((N_EVEN, HY_EMB_DIM, HY_FILTER_HIDDEN), HY_EMB_DIM ** -0.5),
        'hy_b1': nrm((N_EVEN, HY_FILTER_HIDDEN), 0.1),
        'hy_w2': nrm((N_EVEN, HY_FILTER_HIDDEN, HY_FILTER_HIDDEN), HY_FILTER_HIDDEN ** -0.5),
        'hy_b2': nrm((N_EVEN, HY_FILTER_HIDDEN), 0.1),
        'hy_w3': nrm((N_EVEN, HY_FILTER_HIDDEN, HY_FILTER_HIDDEN), HY_FILTER_HIDDEN ** -0.5),
        'hy_b3': nrm((N_EVEN, HY_FILTER_HIDDEN), 0.1),
        'hy_w4': nrm((N_EVEN, HY_FILTER_HIDDEN, 2 * HY_WIDTH), 0.1 * HY_FILTER_HIDDEN ** -0.5),
        'hy_freq': 1.0 + nrm((N_EVEN, 3, HY_FILTER_HIDDEN), 0.01),
        'hy_bias': nrm((N_EVEN, HY_WIDTH), 0.5),
        'odd_w_in': nrm((N_ODD, D, ODD_IN), D ** -0.5),
        'rw_mu': unif((N_ODD, 2, 3 * RW_WIDTH + RW_LR), 0.0, 1.0),
        'rw_w0': unif((N_ODD, 2, RW_WIDTH), -6.0, 1.0),
        'rw_w2': nrm((N_ODD, 2, RW_DECAY_RANK, RW_WIDTH), 0.1 * RW_DECAY_RANK ** -0.5),
        'rw_a0': nrm((N_ODD, 2, RW_WIDTH), 0.1),
        'rw_a2': nrm((N_ODD, 2, RW_ICL_RANK, RW_WIDTH), 0.1 * RW_ICL_RANK ** -0.5),
        'rw_g2': nrm((N_ODD, 2, RW_GATE_RANK, RW_WIDTH), RW_GATE_RANK ** -0.5),
        'rw_k_k': 0.85 + nrm((N_ODD, RW_WIDTH), 0.01),
        'rw_k_a': 1.0 + nrm((N_ODD, RW_WIDTH), 0.01),
        'rw_r_k': nrm((N_ODD, RW_HEADS, RW_HEAD), 0.1),
        'rw_lnx_g': 1.0 + nrm((N_ODD, RW_WIDTH), 0.01),
        'rw_lnx_b': nrm((N_ODD, RW_WIDTH), 0.01),
        'sc_conv_w': nrm((N_ODD, 3, SC_WIDTH), 3 ** -0.5),
    }


def reference(x, c, ctx, c_ctx, ada_w, ada_b, ln1_g, ln1_b, ln2_g, ln2_b, mlp_w_up, mlp_w_down, w_out,
              even_w_in, na_rpb, hy_conv_w, hy_conv_b, hy_w1, hy_b1, hy_w2, hy_b2, hy_w3, hy_b3, hy_w4,
              hy_freq, hy_bias, odd_w_in, rw_mu, rw_w0, rw_w2, rw_a0, rw_a2, rw_g2, rw_k_k, rw_k_a, rw_r_k,
              rw_lnx_g, rw_lnx_b, sc_conv_w):
    silu_c = jax.nn.silu(c)
    silu_cc = jax.nn.silu(c_ctx)
    for l in range(DEPTH):
        need_ctx = l < DEPTH - 1
        i = l // 2
        mx = [m[:, None, :] for m in jnp.split(silu_c @ ada_w[l] + ada_b[l], 6, axis=-1)]
        mc = jnp.split(silu_cc @ ada_w[l] + ada_b[l], 6, axis=-1)
        hx = x * (1.0 + mx[1]) + mx[0]
        hc = ctx * (1.0 + mc[1]) + mc[0]
        if l % 2 == 0:
            filt = (hy_w1[i], hy_b1[i], hy_w2[i], hy_b2[i], hy_w3[i], hy_b3[i], hy_w4[i], hy_freq[i])
            yx, yc = even_mixer(hx, hc, even_w_in[i], na_rpb[i], hy_conv_w[i], hy_conv_b[i], hy_bias[i],
                                filt, need_ctx)
        else:
            yx, yc = odd_mixer(hx, hc, odd_w_in[i], rw_mu[i], rw_w0[i], rw_w2[i], rw_a0[i], rw_a2[i],
                               rw_g2[i], rw_k_k[i], rw_k_a[i], rw_r_k[i], rw_lnx_g[i], rw_lnx_b[i],
                               sc_conv_w[i], need_ctx)
        x = layer_norm(DN_ALPHA * x + mx[2] * (yx @ w_out[l]), ln1_g[l], ln1_b[l])
        x = layer_norm(DN_ALPHA * x + mx[5] * sqrelu_mlp(x * (1.0 + mx[4]) + mx[3], mlp_w_up[l], mlp_w_down[l]),
                       ln2_g[l], ln2_b[l])
        if need_ctx:
            ctx = layer_norm(DN_ALPHA * ctx + mc[2] * (yc @ w_out[l]), ln1_g[l], ln1_b[l])
            ctx = layer_norm(DN_ALPHA * ctx + mc[5] * sqrelu_mlp(ctx * (1.0 + mc[4]) + mc[3], mlp_w_up[l],
                                                                  mlp_w_down[l]), ln2_g[l], ln2_b[l])
    return x
```

```python
import functools
import math

import jax
import jax.numpy as jnp
from jax import lax
from jax.experimental import pallas as pl
from jax.experimental.pallas import tpu as pltpu

F32 = jnp.float32
BF16 = jnp.bfloat16

D_MODEL = 2048
DEPTH = 4
GRID_W = 64
D_GROUP = D_MODEL // 2
NA_HEADS = 8
NA_HEAD_DIM = D_GROUP // NA_HEADS
NA_WIN_H = 8
NA_WIN_W = 16
HY_WIDTH = D_GROUP
HY_EMB_DIM = 33
HY_BANDS = (HY_EMB_DIM - 1) // 2
HY_FILTER_HIDDEN = 64
HY_MIN_DECAY = math.log(1e-2) / 1.5
HY_MAX_DECAY = math.log(1e-2) / 0.3
RW_WIDTH = D_GROUP
RW_HEAD = 64
RW_HEADS = RW_WIDTH // RW_HEAD
RW_DECAY_RANK = 96
RW_ICL_RANK = 96
RW_GATE_RANK = 256
RW_LR = RW_DECAY_RANK + RW_ICL_RANK + RW_GATE_RANK
RW_LR_PAD = 512
RW_GN_EPS = 64e-5
RW_CHUNK = 64
MLP_HIDDEN = 4 * D_MODEL
DN_ALPHA = (2.0 * DEPTH) ** 0.25
LN_EPS = 1e-5
NEG_INF = -1e30
LANES = 128
VMEM_LIMIT = 56 * 1024 * 1024


def _sds(shape, dtype=F32):
    return jax.ShapeDtypeStruct(shape, dtype)


def _params(*sem):
    return pltpu.CompilerParams(dimension_semantics=sem, vmem_limit_bytes=VMEM_LIMIT)


def _dot(a, b):
    return jnp.dot(a.astype(BF16), b.astype(BF16), preferred_element_type=F32)


def _dot_nt(a, b):
    return lax.dot_general(a.astype(BF16), b.astype(BF16), (((1,), (1,)), ((), ())), preferred_element_type=F32)


def _dot_tn(a, b):
    return lax.dot_general(a.astype(BF16), b.astype(BF16), (((0,), (0,)), ((), ())), preferred_element_type=F32)


def _split(x):
    hi = x.astype(BF16)
    lo = (x - hi.astype(F32)).astype(BF16)
    return hi, lo


def _dot3(a, b):
    ah, al = _split(a)
    bh, bl = _split(b)
    d = functools.partial(jnp.dot, preferred_element_type=F32)
    return d(ah, bh) + d(ah, bl) + d(al, bh)


def _dot2(a, b_bf16):
    ah, al = _split(a)
    d = functools.partial(jnp.dot, preferred_element_type=F32)
    return d(ah, b_bf16) + d(al, b_bf16)


def _dot_f32(a, b):
    return jnp.dot(a, b, preferred_element_type=F32, precision=lax.Precision.HIGHEST)


def _layer_norm(t, g, b):
    mu = jnp.mean(t, axis=-1, keepdims=True)
    c = t - mu
    var = jnp.mean(c * c, axis=-1, keepdims=True)
    return c * lax.rsqrt(var + LN_EPS) * g + b


def _mod_spec(j, ctx, nargs):
    if nargs == 2:
        return pl.BlockSpec((1, 1, 1, D_MODEL), lambda b, i: (2 if ctx else b, j, 0, 0))
    return pl.BlockSpec((1, 1, 1, D_MODEL), lambda a, b, i: (2 if ctx else b, j, 0, 0))


def _ada_body(s_ref, w_ref, b_ref, o_ref):
    s = s_ref[...]
    s = s * jax.nn.sigmoid(s)
    o_ref[0] = _dot3(s, w_ref[0]) + b_ref[0]


def _ada_mods(svec, ada_w, ada_b):
    depth, d, n = ada_w.shape
    tn = 1024
    return pl.pallas_call(
        _ada_body, grid=(depth, n // tn),
        in_specs=[pl.BlockSpec((8, d), lambda l, j: (0, 0)),
                  pl.BlockSpec((1, d, tn), lambda l, j: (l, 0, j)),
                  pl.BlockSpec((1, 1, tn), lambda l, j: (l, 0, j))],
        out_specs=pl.BlockSpec((1, 8, tn), lambda l, j: (l, 0, j)),
        out_shape=_sds((depth, 8, n)),
        compiler_params=_params("arbitrary", "arbitrary"), name="ada_mods",
    )(svec, ada_w, ada_b.reshape(depth, 1, n))


def _inproj_body(x_ref, m0_ref, m1_ref, w_ref, o_ref):
    h = x_ref[0] * (1.0 + m1_ref[0, 0]) + m0_ref[0, 0]
    o_ref[0] = jnp.dot(h.astype(BF16), w_ref[...], preferred_element_type=F32)


def _inproj(x, mods, w, ctx, tn):
    bsz, t, d = x.shape
    n = w.shape[1]
    tm = min(t, 512)
    return pl.pallas_call(
        _inproj_body, grid=(n // tn, bsz, t // tm),
        in_specs=[pl.BlockSpec((1, tm, d), lambda a, b, i: (b, i, 0)),
                  _mod_spec(0, ctx, 3), _mod_spec(1, ctx, 3),
                  pl.BlockSpec((d, tn), lambda a, b, i: (0, a))],
        out_specs=pl.BlockSpec((1, tm, tn), lambda a, b, i: (b, i, a)),
        out_shape=_sds((bsz, t, n)),
        compiler_params=_params("arbitrary", "arbitrary", "arbitrary"), name="inproj",
    )(x, mods, mods, w)


def _outproj_body(x_ref, ya_ref, yb_ref, wa_ref, wb_ref, g_ref, lg_ref, lb_ref, o_ref):
    y = (jnp.dot(ya_ref[0].astype(BF16), wa_ref[...], preferred_element_type=F32)
         + jnp.dot(yb_ref[0].astype(BF16), wb_ref[...], preferred_element_type=F32))
    t = DN_ALPHA * x_ref[0] + g_ref[0, 0] * y
    o_ref[0] = _layer_norm(t, lg_ref[...], lb_ref[...])


def _outproj(x, ya, yb, wa, wb, mods, ln_g, ln_b, ctx):
    bsz, t, d = x.shape
    w = ya.shape[2]
    tm = 256
    row = pl.BlockSpec((1, tm, d), lambda b, i: (b, i, 0))
    half = pl.BlockSpec((1, tm, w), lambda b, i: (b, i, 0))
    wsp = pl.BlockSpec((w, d), lambda b, i: (0, 0))
    vec = pl.BlockSpec((1, d), lambda b, i: (0, 0))
    return pl.pallas_call(
        _outproj_body, grid=(bsz, t // tm),
        in_specs=[row, half, half, wsp, wsp, _mod_spec(2, ctx, 2), vec, vec],
        out_specs=row, out_shape=_sds(x.shape),
        compiler_params=_params("arbitrary", "arbitrary"), name="outproj_ln",
    )(x, ya, yb, wa, wb, mods, ln_g, ln_b)


def _mlp_body(nk, x_ref, m3_ref, m4_ref, m5_ref, wu_ref, wd_ref, lg_ref, lb_ref, o_ref, h_scr, acc_scr):
    k = pl.program_id(2)

    @pl.when(k == 0)
    def _():
        h_scr[...] = (x_ref[0] * (1.0 + m4_ref[0, 0]) + m3_ref[0, 0]).astype(BF16)
        acc_scr[...] = jnp.zeros_like(acc_scr)

    u = jnp.dot(h_scr[...], wu_ref[...], preferred_element_type=F32)
    u = jnp.square(jnp.maximum(u, 0.0)).astype(BF16)
    acc_scr[...] += jnp.dot(u, wd_ref[...], preferred_element_type=F32)

    @pl.when(k == nk - 1)
    def _():
        t = DN_ALPHA * x_ref[0] + m5_ref[0, 0] * acc_scr[...]
        o_ref[0] = _layer_norm(t, lg_ref[...], lb_ref[...])


def _mlp(x, mods, w_up, w_down, ln_g, ln_b, ctx):
    bsz, t, d = x.shape
    hid = w_up.shape[1]
    tm = min(t, 512)
    th = 512
    nk = hid // th
    row = pl.BlockSpec((1, tm, d), lambda b, i, k: (b, i, 0))
    vec = pl.BlockSpec((1, d), lambda b, i, k: (0, 0))

    def mod(j):
        return pl.BlockSpec((1, 1, 1, d), lambda b, i, k: (2 if ctx else b, j, 0, 0))

    return pl.pallas_call(
        functools.partial(_mlp_body, nk), grid=(bsz, t // tm, nk),
        in_specs=[row, mod(3), mod(4), mod(5),
                  pl.BlockSpec((d, th), lambda b, i, k: (0, k)),
                  pl.BlockSpec((th, d), lambda b, i, k: (k, 0)), vec, vec],
        out_specs=row, out_shape=_sds(x.shape),
        scratch_shapes=[pltpu.VMEM((tm, d), BF16), pltpu.VMEM((tm, d), F32)],
        compiler_params=_params("arbitrary", "arbitrary", "arbitrary"), name="mlp_ln",
    )(x, mods, mods, mods, w_up, w_down, ln_g, ln_b)


def _na_bias_table(rpb):
    cols = jnp.arange(GRID_W)
    col0 = jnp.clip(cols - NA_WIN_W // 2, 0, GRID_W - NA_WIN_W)
    col_ok = (cols[None, :] >= col0[:, None]) & (cols[None, :] < col0[:, None] + NA_WIN_W)
    d_col = jnp.clip(cols[None, :] - cols[:, None] + (NA_WIN_W - 1), 0, 2 * NA_WIN_W - 2)
    d_row = jnp.arange(NA_WIN_H)[:, None] + jnp.arange(NA_WIN_H)[None, :]
    tab = rpb[:, d_row[:, None, :, None], d_col[None, :, None, :]].astype(F32)
    tab = jnp.where(col_ok[None, None, :, None, :], tab, NEG_INF)
    return tab.reshape(rpb.shape[0], NA_WIN_H, GRID_W, NA_WIN_H * GRID_W)


def _na_body(rows_per_step, n_rows, q_ref, k_ref, v_ref, kc_ref, vc_ref, tb_ref, o_ref, kb_scr, vb_scr):
    i = pl.program_id(2)

    @pl.when(i == 0)
    def _():
        kb_scr[...] = k_ref[0].astype(BF16)
        vb_scr[...] = v_ref[0].astype(BF16)

    scale = NA_HEAD_DIM ** -0.5
    kc = kc_ref[0].astype(BF16)
    vc = vc_ref[0].astype(BF16)
    band = NA_WIN_H * GRID_W

    def one_row(rr, carry):
        r = i * rows_per_step + rr
        start = jnp.clip(r - NA_WIN_H // 2, 0, n_rows - NA_WIN_H)
        off = start - r + (NA_WIN_H - 1)
        q0 = pl.multiple_of(rr * GRID_W, GRID_W)
        k0 = pl.multiple_of(start * GRID_W, GRID_W)
        q = q_ref[0, pl.ds(q0, GRID_W), :].astype(BF16)
        kb = kb_scr[pl.ds(k0, band), :]
        vb = vb_scr[pl.ds(k0, band), :]
        s_loc = _dot_nt(q, kb) * scale + tb_ref[0, off]
        s_ctx = _dot_nt(q, kc) * scale
        m = jnp.maximum(jnp.max(s_loc, axis=-1, keepdims=True), jnp.max(s_ctx, axis=-1, keepdims=True))
        p_loc = jnp.exp(s_loc - m)
        p_ctx = jnp.exp(s_ctx - m)
        den = jnp.sum(p_loc, axis=-1, keepdims=True) + jnp.sum(p_ctx, axis=-1, keepdims=True)
        o = _dot(p_loc, vb) + _dot(p_ctx, vc)
        o_ref[0, pl.ds(q0, GRID_W), :] = o / den
        return carry

    lax.fori_loop(0, rows_per_step, one_row, 0)


def _na_attention(zx, zc, table):
    bsz, n, _ = zx.shape
    nc = zc.shape[1]
    n_rows = n // GRID_W
    rps = min(8, n_rows)
    hd = NA_HEAD_DIM
    return pl.pallas_call(
        functools.partial(_na_body, rps, n_rows), grid=(bsz, NA_HEADS, n_rows // rps),
        in_specs=[pl.BlockSpec((1, rps * GRID_W, hd), lambda b, h, i: (b, i, h)),
                  pl.BlockSpec((1, n, hd), lambda b, h, i: (b, 0, NA_HEADS + h)),
                  pl.BlockSpec((1, n, hd), lambda b, h, i: (b, 0, 2 * NA_HEADS + h)),
                  pl.BlockSpec((1, nc, hd), lambda b, h, i: (b, 0, NA_HEADS + h)),
                  pl.BlockSpec((1, nc, hd), lambda b, h, i: (b, 0, 2 * NA_HEADS + h)),
                  pl.BlockSpec((1, NA_WIN_H, GRID_W, NA_WIN_H * GRID_W), lambda b, h, i: (h, 0, 0, 0))],
        out_specs=pl.BlockSpec((1, rps * GRID_W, hd), lambda b, h, i: (b, i, h)),
        out_shape=_sds((bsz, n, D_GROUP)),
        scratch_shapes=[pltpu.VMEM((n, hd), BF16), pltpu.VMEM((n, hd), BF16)],
        compiler_params=_params("arbitrary", "arbitrary", "arbitrary"), name="na_attention",
    )(zx, zx, zx, zc, zc, table)


def _ctx_attn_body(q_ref, k_ref, v_ref, o_ref):
    s = _dot_nt(q_ref[0], k_ref[0]) * (NA_HEAD_DIM ** -0.5)
    m = jnp.max(s, axis=-1, keepdims=True)
    p = jnp.exp(s - m)
    o_ref[0] = _dot(p, v_ref[0]) / jnp.sum(p, axis=-1, keepdims=True)


def _ctx_attention(zc):
    bsz, nc, _ = zc.shape
    hd = NA_HEAD_DIM

    def spec(o):
        return pl.BlockSpec((1, nc, hd), lambda b, h: (b, 0, o + h))

    return pl.pallas_call(
        _ctx_attn_body, grid=(bsz, NA_HEADS),
        in_specs=[spec(0), spec(NA_HEADS), spec(2 * NA_HEADS)], out_specs=spec(0),
        out_shape=_sds((bsz, nc, D_GROUP)),
        compiler_params=_params("arbitrary", "arbitrary"), name="ctx_attention",
    )(zc, zc, zc)


def _conv3(x, w):
    n = x.shape[0]
    row = lax.broadcasted_iota(jnp.int32, x.shape, 0)
    prev = jnp.where(row == 0, 0.0, pltpu.roll(x, 1, 0))
    nxt = jnp.where(row == n - 1, 0.0, pltpu.roll(x, n - 1, 0))
    return prev * w[0:1] + x * w[1:2] + nxt * w[2:3]


def _hy_pre_body(x0_ref, x1_ref, v_ref, w0_ref, w1_ref, w2_ref, b0_ref, b1_ref, b2_ref, g_ref, gb_ref, x0o_ref):
    x0 = _conv3(x0_ref[0], w0_ref[...]) + b0_ref[...]
    x1 = _conv3(x1_ref[0], w1_ref[...]) + b1_ref[...]
    v = _conv3(v_ref[0], w2_ref[...]) + b2_ref[...]
    g = v * x1
    g_ref[...] = g
    gb_ref[...] = g.astype(BF16)
    x0o_ref[0] = x0


def _hy_pre(z, conv_w, conv_b):
    bsz, n, _ = z.shape
    nb = HY_WIDTH // LANES
    base = 3 * D_GROUP // LANES

    def zs(o):
        return pl.BlockSpec((1, n, LANES), lambda b, c: (b, 0, base + o * nb + c))

    def ws(o):
        return pl.BlockSpec((3, LANES), lambda b, c: (0, o * nb + c))

    def bs(o):
        return pl.BlockSpec((1, LANES), lambda b, c: (0, o * nb + c))

    gspec = pl.BlockSpec((n, LANES), lambda b, c: (0, b * nb + c))
    return pl.pallas_call(
        _hy_pre_body, grid=(bsz, nb),
        in_specs=[zs(0), zs(1), zs(2), ws(0), ws(1), ws(2), bs(0), bs(1), bs(2)],
        out_specs=[gspec, gspec, pl.BlockSpec((1, n, LANES), lambda b, c: (b, 0, c))],
        out_shape=[_sds((n, bsz * HY_WIDTH)), _sds((n, bsz * HY_WIDTH), BF16), _sds((bsz, n, HY_WIDTH))],
        compiler_params=_params("arbitrary", "arbitrary"), name="hyena_pre",
    )(z, z, z, conv_w, conv_w, conv_w, conv_b, conv_b, conv_b)


def _hy_filter_body(tl, z_ref, w1_ref, b1_ref, w2_ref, b2_ref, w3_ref, b3_ref, w4_ref, fq_ref, dec_ref,
                    hs_ref, hd_ref):
    z = z_ref[...]
    h = jnp.sin(fq_ref[0:1] * (_dot_f32(z, w1_ref[...]) + b1_ref[...]))
    h = jnp.sin(fq_ref[1:2] * (_dot_f32(h, w2_ref[...]) + b2_ref[...]))
    h = jnp.sin(fq_ref[2:3] * (_dot_f32(h, w3_ref[...]) + b3_ref[...]))
    h4 = _dot_f32(h, w4_ref[...])
    win = jnp.exp(-z[:, 0:1] * dec_ref[...])
    hf = h4[:, :HY_WIDTH] * win
    hb = h4[:, HY_WIDTH:] * win
    row = lax.broadcasted_iota(jnp.int32, hb.shape, 0) + pl.program_id(0) * tl
    hb = jnp.where(row == 0, 0.0, hb)
    hs_ref[...] = (hf + hb).astype(BF16)
    hd_ref[...] = (hb - hf).astype(BF16)


def _hy_filter(n, w1, b1, w2, b2, w3, b3, w4, freq):
    f = HY_FILTER_HIDDEN
    t = jnp.linspace(0.0, 1.0, n, dtype=F32)[:, None]
    bands = jnp.linspace(1e-4, HY_BANDS - 1, HY_BANDS, dtype=F32)
    ang = (2.0 * math.pi / n) * jnp.arange(n, dtype=F32)[:, None] * bands[None, :]
    emb = jnp.concatenate([t, jnp.cos(ang), -jnp.sin(ang), jnp.zeros((n, f - HY_EMB_DIM), F32)], axis=-1)
    w1p = jnp.concatenate([w1.astype(F32), jnp.zeros((f - HY_EMB_DIM, f), F32)], axis=0)
    decay = jnp.abs(jnp.linspace(HY_MIN_DECAY, HY_MAX_DECAY, HY_WIDTH, dtype=F32))[None, :]
    tl = min(n, 512)

    def full(shape):
        return pl.BlockSpec(shape, lambda i: (0, 0))

    out = pl.BlockSpec((tl, HY_WIDTH), lambda i: (i, 0))
    return pl.pallas_call(
        functools.partial(_hy_filter_body, tl), grid=(n // tl,),
        in_specs=[pl.BlockSpec((tl, f), lambda i: (i, 0)), full((f, f)), full((1, f)), full((f, f)), full((1, f)),
                  full((f, f)), full((1, f)), full((f, 2 * HY_WIDTH)), full((3, f)), full((1, HY_WIDTH))],
        out_specs=[out, out], out_shape=[_sds((n, HY_WIDTH), BF16)] * 2,
        compiler_params=_params("arbitrary"), name="hyena_filter",
    )(emb, w1p, b1.astype(F32)[None], w2.astype(F32), b2.astype(F32)[None], w3.astype(F32), b3.astype(F32)[None],
      w4.astype(F32), freq.astype(F32), decay)


def _dft_tables(n):
    k = jnp.arange(n, dtype=jnp.int32)
    m = ((2 * k[:, None] + 1) * k[None, :]) % (4 * n)
    ang = m.astype(F32) * (math.pi / (2 * n))
    cm = jnp.cos(ang).astype(BF16)
    sm = jnp.sin(ang).astype(BF16)
    return cm, sm, cm.T, sm.T


def _dft_filter_body(cm_ref, sm_ref, hs_ref, hd_ref, fre_ref, fim_ref):
    fre_ref[...] = jnp.dot(cm_ref[...], hs_ref[...], preferred_element_type=F32)
    fim_ref[...] = jnp.dot(sm_ref[...], hd_ref[...], preferred_element_type=F32)


def _dft_fwd_body(cm_ref, sm_ref, g_ref, fre_ref, fim_ref, yre_ref, yim_ref):
    p = jnp.dot(cm_ref[...], g_ref[...], preferred_element_type=F32)
    q = jnp.dot(sm_ref[...], g_ref[...], preferred_element_type=F32)
    fre = fre_ref[...]
    fim = fim_ref[...]
    yre_ref[...] = (p * fre + q * fim).astype(BF16)
    yim_ref[...] = (p * fim - q * fre).astype(BF16)


def _dft_inv_body(scale, cmt_ref, smt_ref, yre_ref, yim_ref, g_ref, bias_ref, x0_ref, o_ref):
    y = (jnp.dot(cmt_ref[...], yre_ref[...], preferred_element_type=F32)
         - jnp.dot(smt_ref[...], yim_ref[...], preferred_element_type=F32)) * scale
    o_ref[0] = (y + g_ref[...] * bias_ref[...]) * x0_ref[0]


def _hyena(z, conv_w, conv_b, bias, filt, tables):
    bsz, n, _ = z.shape
    cm, sm, cmt, smt = tables
    g, gb, x0 = _hy_pre(z, conv_w, conv_b)
    hs, hd = _hy_filter(n, *filt)
    tk = min(n, 512)
    tc = 512
    ncb = HY_WIDTH // tc
    cols = bsz * HY_WIDTH
    tab = pl.BlockSpec((tk, n), lambda i, j: (i, 0))
    colblk = pl.BlockSpec((n, tc), lambda i, j: (0, j))
    tile = pl.BlockSpec((tk, tc), lambda i, j: (i, j))
    ftile = pl.BlockSpec((tk, tc), lambda i, j: (i, j % ncb))
    cp = _params("arbitrary", "arbitrary")
    fre, fim = pl.pallas_call(
        _dft_filter_body, grid=(n // tk, ncb), in_specs=[tab, tab, colblk, colblk],
        out_specs=[tile, tile], out_shape=[_sds((n, HY_WIDTH))] * 2, compiler_params=cp, name="hyena_dft_filter",
    )(cm, sm, hs, hd)
    yre, yim = pl.pallas_call(
        _dft_fwd_body, grid=(n // tk, cols // tc), in_specs=[tab, tab, colblk, ftile, ftile],
        out_specs=[tile, tile], out_shape=[_sds((n, cols), BF16)] * 2, compiler_params=cp, name="hyena_dft_fwd",
    )(cm, sm, gb, fre, fim)
    bt = pl.BlockSpec((1, tk, tc), lambda i, j: (j // ncb, i, j % ncb))
    return pl.pallas_call(
        functools.partial(_dft_inv_body, 1.0 / n), grid=(n // tk, cols // tc),
        in_specs=[tab, tab, colblk, colblk, tile, pl.BlockSpec((1, tc), lambda i, j: (0, j % ncb)), bt],
        out_specs=bt, out_shape=_sds((bsz, n, HY_WIDTH)), compiler_params=cp, name="hyena_dft_inv",
    )(cmt, smt, yre, yim, g, bias.astype(F32)[None], x0)


def _rw_prep_body(rev, nblk, z_ref, zl_ref, hz_ref, hzl_ref, mu_ref, mul_ref, w0_ref, w2_ref, a0_ref, a2_ref,
                  g2_ref, kk_ref, ka_ref, rk_ref, gm_ref,
                  r_o, k_o, v_o, lw_o, kn_o, b_o, gate_o, bonus_o):
    i = pl.program_id(1)

    def mix(z, halo, mu):
        tb = z.shape[0]
        row = lax.broadcasted_iota(jnp.int32, z.shape, 0)
        if rev:
            edge = jnp.where(i == nblk - 1, 0.0, halo[0:1])
            zs = jnp.where(row == tb - 1, edge, pltpu.roll(z, tb - 1, 0))
        else:
            edge = jnp.where(i == 0, 0.0, halo[7:8])
            zs = jnp.where(row == 0, edge, pltpu.roll(z, 1, 0))
        return z + mu * (zs - z)

    z = mix(z_ref[0], hz_ref[0], mu_ref[...])
    zl = mix(zl_ref[0], hzl_ref[0], mul_ref[...])
    wd = RW_WIDTH
    r, k, v = z[:, :wd], z[:, wd:2 * wd], z[:, 2 * wd:]
    wl, al, gl = zl[:, :LANES], zl[:, LANES:2 * LANES], zl[:, 2 * LANES:]
    w = -jax.nn.softplus(-(w0_ref[...] + _dot3(jnp.tanh(wl), w2_ref[...]))) - 0.5
    a = jax.nn.sigmoid(a0_ref[...] + _dot3(al, a2_ref[...]))
    gm = gm_ref[...]
    kkv = k * kk_ref[...]
    kn = kkv / jnp.maximum(jnp.sqrt(_dot2(kkv * kkv, gm)), 1e-12)
    k2 = k * (1.0 + (a - 1.0) * ka_ref[...])
    r_o[0] = r
    k_o[0] = k2
    v_o[0] = v
    lw_o[0] = -jnp.exp(w)
    kn_o[0] = kn
    b_o[0] = kn * a
    gate_o[0] = _dot3(jax.nn.sigmoid(gl), g2_ref[...])
    bonus_o[0] = _dot2(r * k2 * rk_ref[...], gm) * v


def _rw_prep(z, d, prm):
    bsz, t, _ = z.shape
    tb = 256
    nblk = t // tb
    rev = d == 1
    w3 = 3 * RW_WIDTH
    lr_blk = w3 // RW_LR_PAD + d
    hb = tb // 8
    nh = t // 8

    def halo(b, i):
        return jnp.minimum((i + 1) * hb, nh - 1) if rev else jnp.maximum(i * hb - 1, 0)

    def full(shape):
        return pl.BlockSpec(shape, lambda b, i: (0,) * len(shape))

    out = pl.BlockSpec((1, tb, RW_WIDTH), lambda b, i: (b, i, 0))
    return pl.pallas_call(
        functools.partial(_rw_prep_body, rev, nblk), grid=(bsz, nblk),
        in_specs=[pl.BlockSpec((1, tb, w3), lambda b, i: (b, i, 0)),
                  pl.BlockSpec((1, tb, RW_LR_PAD), lambda b, i: (b, i, lr_blk)),
                  pl.BlockSpec((1, 8, w3), lambda b, i: (b, halo(b, i), 0)),
                  pl.BlockSpec((1, 8, RW_LR_PAD), lambda b, i: (b, halo(b, i), lr_blk)),
                  full((1, w3)), full((1, RW_LR_PAD)), full((1, RW_WIDTH)), full((LANES, RW_WIDTH)),
                  full((1, RW_WIDTH)), full((LANES, RW_WIDTH)), full((RW_GATE_RANK, RW_WIDTH)),
                  full((1, RW_WIDTH)), full((1, RW_WIDTH)), full((1, RW_WIDTH)), full((RW_WIDTH, RW_WIDTH))],
        out_specs=[out] * 8, out_shape=[_sds((bsz, t, RW_WIDTH))] * 8,
        compiler_params=_params("arbitrary", "arbitrary"), name="rwkv_prep",
    )(z, z, z, z, prm["mu"][d], prm["mul"][d], prm["w0"][d], prm["w2"][d], prm["a0"][d], prm["a2"][d],
      prm["g2"][d], prm["k_k"], prm["k_a"], prm["r_k"], prm["gsum"])


def _rw_scan_body(rev, nchunk, r_ref, k_ref, v_ref, lw_ref, kn_ref, b_ref, s0_ref, y_ref, sT_ref, s_scr):
    c = pl.program_id(1)
    cl = RW_CHUNK

    @pl.when(c == 0)
    def _():
        s_scr[...] = s0_ref[0]

    ti = lax.broadcasted_iota(jnp.int32, (2 * cl, 2 * cl), 0)
    tj = lax.broadcasted_iota(jnp.int32, (2 * cl, 2 * cl), 1)
    same = (ti & cl) == (tj & cl)
    ti, tj = ti & (cl - 1), tj & (cl - 1)
    strict = same & ((tj > ti) if rev else (tj < ti))
    incl = same & ((tj >= ti) if rev else (tj <= ti))
    eye = jnp.where(lax.broadcasted_iota(jnp.int32, (2 * cl, 2 * cl), 0)
                    == lax.broadcasted_iota(jnp.int32, (2 * cl, 2 * cl), 1), 1.0, 0.0)
    ci = lax.broadcasted_iota(jnp.int32, (cl, cl), 0)
    cj = lax.broadcasted_iota(jnp.int32, (cl, cl), 1)
    csum = jnp.where((cj >= ci) if rev else (cj <= ci), 1.0, 0.0)
    head_a = lax.broadcasted_iota(jnp.int32, (cl, LANES), 1) < RW_HEAD

    def stack(x):
        return jnp.concatenate([jnp.where(head_a, x, 0.0), jnp.where(head_a, 0.0, x)], axis=0)

    def pair(p, carry):
        sl = pl.ds(pl.multiple_of(p * LANES, LANES), LANES)
        r, k, v = r_ref[0, :, sl], k_ref[0, :, sl], v_ref[0, :, sl]
        lw, kn, b = lw_ref[0, :, sl], kn_ref[0, :, sl], b_ref[0, :, sl]
        cum = _dot_f32(csum, lw)
        tot = jnp.sum(lw, axis=0, keepdims=True)
        rem = jnp.exp(tot - cum)
        inv = jnp.exp(-cum)
        khs = stack(kn * jnp.exp(cum - lw))
        rhs = stack(r * jnp.exp(cum))
        bts = stack(b * inv)
        kts = stack(k * inv)
        vs = stack(v)
        lb = jnp.where(strict, _dot_nt(khs, bts), 0.0)
        lk = jnp.where(strict, _dot_nt(khs, kts), 0.0)
        mrb = jnp.where(incl, _dot_nt(rhs, bts), 0.0)
        mrk = jnp.where(incl, _dot_nt(rhs, kts), 0.0)
        tinv = eye - lb
        pw = lb
        for _ in range(5):
            pw = _dot3(pw, pw)
            tinv = tinv + _dot3(tinv, pw)
        s = s_scr[p]
        u = _dot3(tinv, -(_dot_nt(khs, s) + _dot(lk, vs)))
        y = _dot_nt(rhs, s) + _dot(mrb, u) + _dot(mrk, vs)
        y_ref[0, :, sl] = y[:cl] + y[cl:]
        s_scr[p] = s * jnp.exp(tot) + _dot_tn(u, stack(b * rem)) + _dot_tn(vs, stack(k * rem))
        return carry

    lax.fori_loop(0, RW_HEADS // 2, pair, 0, unroll=2)

    @pl.when(c == nchunk - 1)
    def _():
        sT_ref[0] = s_scr[...]


def _rw_scan(seq, s0, rev):
    bsz, t, _ = seq[0].shape
    nchunk = t // RW_CHUNK
    npair = RW_HEADS // 2
    blk = pl.BlockSpec((1, RW_CHUNK, RW_WIDTH), lambda b, c: (b, (nchunk - 1 - c) if rev else c, 0))
    st = pl.BlockSpec((1, npair, LANES, LANES), lambda b, c: (b, 0, 0, 0))
    return pl.pallas_call(
        functools.partial(_rw_scan_body, rev, nchunk), grid=(bsz, nchunk),
        in_specs=[blk] * 6 + [st], out_specs=[blk, st],
        out_shape=[_sds((bsz, t, RW_WIDTH)), _sds((bsz, npair, LANES, LANES))],
        scratch_shapes=[pltpu.VMEM((npair, LANES, LANES), F32)],
        compiler_params=_params("arbitrary", "arbitrary"), name="rwkv_scan",
    )(*seq, s0)


def _rw_out_body(yf_ref, bf_ref, gf_ref, yb_ref, bb_ref, gb_ref, lg_ref, lb_ref, gm_ref, o_ref):
    gm = gm_ref[...]
    lg = lg_ref[...]
    lb = lb_ref[...]

    def one(y, bonus, gate):
        c = y - _dot2(y, gm)
        var = _dot2(c * c, gm)
        return (c * lax.rsqrt(var + RW_GN_EPS) * lg + lb + bonus) * gate

    o_ref[0] = one(yf_ref[0], bf_ref[0], gf_ref[0]) + one(yb_ref[0], bb_ref[0], gb_ref[0])


def _rw_out(yf, bonf, gatef, yb, bonb, gateb, prm):
    bsz, t, w = yf.shape
    tb = 256
    blk = pl.BlockSpec((1, tb, w), lambda b, i: (b, i, 0))
    vec = pl.BlockSpec((1, w), lambda b, i: (0, 0))
    return pl.pallas_call(
        _rw_out_body, grid=(bsz, t // tb),
        in_specs=[blk] * 6 + [vec, vec, pl.BlockSpec((w, w), lambda b, i: (0, 0))],
        out_specs=blk, out_shape=_sds((bsz, t, w)),
        compiler_params=_params("arbitrary", "arbitrary"), name="rwkv_out",
    )(yf, bonf, gatef, yb, bonb, gateb, prm["lnx_g"], prm["lnx_b"], prm["gmean"])


def _rwkv_bidir(zx, zc, prm, need_ctx):
    npair = RW_HEADS // 2
    outs_x, outs_c = [], []
    for d in range(2):
        rev = d == 1
        pc = _rw_prep(zc, d, prm)
        px = _rw_prep(zx, d, prm)
        s0 = jnp.zeros((zc.shape[0], npair, LANES, LANES), F32)
        yc, s_ctx = _rw_scan(pc[:6], s0, rev)
        yx, _ = _rw_scan(px[:6], s_ctx, rev)
        outs_x += [yx, px[7], px[6]]
        outs_c += [yc, pc[7], pc[6]]
    out_x = _rw_out(*outs_x, prm)
    out_c = _rw_out(*outs_c, prm) if need_ctx else None
    return out_x, out_c


def _sc_body(bg_ref, cg_ref, h_ref, w_ref, o_ref):
    o_ref[0] = bg_ref[0] * _conv3(cg_ref[0] * h_ref[0], w_ref[...])


def _short_conv(z, w, base):
    bsz, t, _ = z.shape
    nb = D_GROUP // LANES
    b0 = base // LANES

    def zs(o):
        return pl.BlockSpec((1, t, LANES), lambda b, c: (b, 0, b0 + o * nb + c))

    return pl.pallas_call(
        _sc_body, grid=(bsz, nb),
        in_specs=[zs(0), zs(1), zs(2), pl.BlockSpec((3, LANES), lambda b, c: (0, c))],
        out_specs=pl.BlockSpec((1, t, LANES), lambda b, c: (b, 0, c)),
        out_shape=_sds((bsz, t, D_GROUP)),
        compiler_params=_params("arbitrary", "arbitrary"), name="short_conv",
    )(z, z, z, w)


def _pad_cols(a, width):
    return jnp.pad(a, [(0, 0)] * (a.ndim - 1) + [(0, width - a.shape[-1])])


def _odd_w_in_padded(w):
    w3 = 3 * RW_WIDTH
    parts = [w[:, :w3]]
    for d in range(2):
        lo = w3 + d * RW_LR
        parts += [_pad_cols(w[:, lo:lo + RW_DECAY_RANK], LANES),
                  _pad_cols(w[:, lo + RW_DECAY_RANK:lo + RW_DECAY_RANK + RW_ICL_RANK], LANES),
                  w[:, lo + RW_DECAY_RANK + RW_ICL_RANK:lo + RW_LR]]
    parts.append(w[:, w3 + 2 * RW_LR:])
    return jnp.concatenate(parts, axis=1).astype(BF16)


def _rw_params(mu, w0, w2, a0, a2, g2, k_k, k_a, r_k, lnx_g, lnx_b):
    w3 = 3 * RW_WIDTH
    f = lambda a: a.astype(F32)
    mu = f(mu)
    mul = jnp.concatenate([_pad_cols(mu[:, w3:w3 + RW_DECAY_RANK], LANES),
                           _pad_cols(mu[:, w3 + RW_DECAY_RANK:w3 + RW_DECAY_RANK + RW_ICL_RANK], LANES),
                           mu[:, w3 + RW_DECAY_RANK + RW_ICL_RANK:]], axis=1)
    pad_rows = lambda a: jnp.pad(f(a), ((0, 0), (0, LANES - a.shape[1]), (0, 0)))
    head = jnp.arange(RW_WIDTH) // RW_HEAD
    same = (head[:, None] == head[None, :])
    return dict(mu=mu[:, None, :w3], mul=mul[:, None, :], w0=f(w0)[:, None, :], w2=pad_rows(w2),
                a0=f(a0)[:, None, :], a2=pad_rows(a2), g2=f(g2), k_k=f(k_k)[None], k_a=f(k_a)[None],
                r_k=f(r_k).reshape(1, RW_WIDTH), lnx_g=f(lnx_g)[None], lnx_b=f(lnx_b)[None],
                gsum=same.astype(BF16), gmean=(same.astype(F32) / RW_HEAD).astype(BF16))


def kernel(x, c, ctx, c_ctx, ada_w, ada_b, ln1_g, ln1_b, ln2_g, ln2_b, mlp_w_up, mlp_w_down, w_out, even_w_in, na_rpb, hy_conv_w, hy_conv_b, hy_w1, hy_b1, hy_w2, hy_b2, hy_w3, hy_b3, hy_w4, hy_freq, hy_bias, odd_w_in, rw_mu, rw_w0, rw_w2, rw_a0, rw_a2, rw_g2, rw_k_k, rw_k_a, rw_r_k, rw_lnx_g, rw_lnx_b, sc_conv_w):
    d = D_MODEL
    svec = jnp.concatenate([c.astype(F32), c_ctx.astype(F32)[None], jnp.zeros((5, d), F32)], axis=0)
    mods_all = _ada_mods(svec, ada_w.astype(F32), ada_b.astype(F32)).reshape(DEPTH, 8, 6, 1, d)
    tables_x = _dft_tables(x.shape[1])
    tables_c = _dft_tables(ctx.shape[1])
    xs, cs = x.astype(F32), ctx.astype(F32)
    for l in range(DEPTH):
        need_ctx = l < DEPTH - 1
        i = l // 2
        mods = mods_all[l]
        if l % 2 == 0:
            w_in = even_w_in[i].astype(BF16)
            zx = _inproj(xs, mods, w_in, False, 1536)
            zc = _inproj(cs, mods, w_in, True, 1536)
            filt = (hy_w1[i], hy_b1[i], hy_w2[i], hy_b2[i], hy_w3[i], hy_b3[i], hy_w4[i], hy_freq[i])
            cw, cb = hy_conv_w[i].astype(F32), hy_conv_b[i].astype(F32)[None]
            ya = _na_attention(zx, zc, _na_bias_table(na_rpb[i]))
            yb = _hyena(zx, cw, cb, hy_bias[i], filt, tables_x)
            if need_ctx:
                ca = _ctx_attention(zc)
                cb_ = _hyena(zc, cw, cb, hy_bias[i], filt, tables_c)
        else:
            w_in = _odd_w_in_padded(odd_w_in[i])
            prm = _rw_params(rw_mu[i], rw_w0[i], rw_w2[i], rw_a0[i], rw_a2[i], rw_g2[i], rw_k_k[i], rw_k_a[i],
                             rw_r_k[i], rw_lnx_g[i], rw_lnx_b[i])
            zx = _inproj(xs, mods, w_in, False, 1792)
            zc = _inproj(cs, mods, w_in, True, 1792)
            sc_base = 3 * RW_WIDTH + 2 * RW_LR_PAD
            ya, ca = _rwkv_bidir(zx, zc, prm, need_ctx)
            scw = sc_conv_w[i].astype(F32)
            yb = _short_conv(zx, scw, sc_base)
            if need_ctx:
                cb_ = _short_conv(zc, scw, sc_base)
        wo = w_out[l].astype(BF16)
        wa, wb = wo[:D_GROUP], wo[D_GROUP:]
        g1, b1 = ln1_g[l].astype(F32)[None], ln1_b[l].astype(F32)[None]
        g2, b2 = ln2_g[l].astype(F32)[None], ln2_b[l].astype(F32)[None]
        wu, wdn = mlp_w_up[l].astype(BF16), mlp_w_down[l].astype(BF16)
        xs = _outproj(xs, ya, yb, wa, wb, mods, g1, b1, False)
        xs = _mlp(xs, mods, wu, wdn, g2, b2, False)
        if need_ctx:
            cs = _outproj(cs, ca, cb_, wa, wb, mods, g1, b1, True)
            cs = _mlp(cs, mods, wu, wdn, g2, b2, True)
    return xs.astype(x.dtype)
```
